```python
import jax, jax.numpy as jnp
from jax import lax
import numpy as np

D_MODEL = 2048
BATCH = 8
SEQ = 2048
DEPTH = 1
DEC_BATCH = 32
DEC_SEQ = 1
PAST_LEN = 16384
PAGE_SIZE = 128

CONV_CH = D_MODEL // 2
CONV_WIDTH = 31
N_HEADS = 8
HEAD_DIM = 128
ATTN_W = N_HEADS * HEAD_DIM
MOBA_BLOCK = 256
MOBA_TOPK = 3
Q_CHUNK = 16
N_MEM = 256
MEM_HEADS = 4
MEM_HEAD_DIM = D_MODEL // 8
MEM_W = MEM_HEADS * MEM_HEAD_DIM
N_BRANCH = 3
D_FF = 4 * D_MODEL
EPS = 1e-6
NEG = -1e30
IN_SIZES = (CONV_CH, CONV_CH, ATTN_W, ATTN_W, ATTN_W, MEM_W, D_MODEL, D_MODEL, D_MODEL)
D_IN = sum(IN_SIZES)

kernel_name = 'hybrid_conformer_moba_memory_decoder_step'


def rmsnorm(x, g):
    xf = x.astype(jnp.float32)
    y = xf * lax.rsqrt(jnp.mean(xf * xf, axis=-1, keepdims=True) + EPS)
    return (y * g.astype(jnp.float32)).astype(x.dtype)


def layernorm(x, g, b):
    xf = x.astype(jnp.float32)
    mu = jnp.mean(xf, axis=-1, keepdims=True)
    var = jnp.mean(jnp.square(xf - mu), axis=-1, keepdims=True)
    y = (xf - mu) * lax.rsqrt(var + EPS)
    return (y * g.astype(jnp.float32) + b.astype(jnp.float32)).astype(x.dtype)


def alibi_slopes():
    return jnp.asarray(2.0 ** (-8.0 * np.arange(1, N_HEADS + 1) / N_HEADS), dtype=jnp.float32)


def depthwise_causal_conv(u_hist, w_dw, b_dw):
    out = lax.conv_general_dilated(u_hist, w_dw[:, None, :].astype(u_hist.dtype), window_strides=(1,),
                                   padding='VALID', dimension_numbers=('NWC', 'WIO', 'NWC'),
                                   feature_group_count=CONV_CH)
    return out + b_dw


def mem_attend(q, mk, mv):
    lg = jnp.einsum('bthd,bmhd->bthm', q, mk, preferred_element_type=jnp.float32) * (MEM_HEAD_DIM ** -0.5)
    p = jax.nn.softmax(lg, axis=-1)
    return jnp.einsum('bthm,bmhd->bthd', p.astype(mv.dtype), mv, preferred_element_type=jnp.float32).astype(q.dtype)


def moba_core(q, t_q, k_sel, v_sel, sel_pos, sel_ok, k_loc, v_loc, loc_pos, loc_ok):
    B, T, H, _ = q.shape
    J, KB = k_sel.shape[3], k_sel.shape[4]
    m = alibi_slopes()
    scale = HEAD_DIM ** -0.5
    lg_sel = jnp.einsum('bthd,bthjkd->bthjk', q, k_sel, preferred_element_type=jnp.float32) * scale
    lg_sel = lg_sel - m[:, None, None] * (t_q[:, None, None, None] - sel_pos).astype(jnp.float32)
    lg_sel = jnp.where(sel_ok, lg_sel, NEG)
    lg_loc = jnp.einsum('bthd,bhld->bthl', q, k_loc, preferred_element_type=jnp.float32) * scale
    lg_loc = lg_loc - m[:, None] * (t_q[:, None, None] - loc_pos).astype(jnp.float32)
    lg_loc = jnp.where(loc_ok, lg_loc, NEG)
    p = jax.nn.softmax(jnp.concatenate([lg_sel.reshape(B, T, H, J * KB), lg_loc], axis=-1), axis=-1)
    p_sel = p[..., :J * KB].reshape(B, T, H, J, KB)
    p_loc = p[..., J * KB:]
    out = (jnp.einsum('bthjk,bthjkd->bthd', p_sel.astype(v_sel.dtype), v_sel, preferred_element_type=jnp.float32)
           + jnp.einsum('bthl,bhld->bthd', p_loc.astype(v_loc.dtype), v_loc, preferred_element_type=jnp.float32))
    return out.astype(q.dtype)


def moba_prompt(q, k, v):
    B, S = q.shape[0], q.shape[1]
    nb = -(-S // MOBA_BLOCK)
    pad = ((0, 0), (0, nb * MOBA_BLOCK - S), (0, 0), (0, 0))
    kb = jnp.pad(k, pad).reshape(B, nb, MOBA_BLOCK, N_HEADS, HEAD_DIM).transpose(0, 3, 1, 2, 4)
    vb = jnp.pad(v, pad).reshape(B, nb, MOBA_BLOCK, N_HEADS, HEAD_DIM).transpose(0, 3, 1, 2, 4)
    J = min(MOBA_TOPK, nb - 1)
    t = jnp.arange(S)
    if J > 0:
        kmean = jnp.mean(kb.astype(jnp.float32), axis=3)
        scores = jnp.einsum('bshd,bhnd->bshn', q, kmean, preferred_element_type=jnp.float32)
        past = jnp.arange(nb)[None, :] < (t // MOBA_BLOCK)[:, None]
        scores = jnp.where(past[None, :, None, :], scores, -jnp.inf)
        _, idx = lax.top_k(scores, J)
    else:
        idx = jnp.zeros((B, S, N_HEADS, 0), jnp.int32)
    n_ch = S // Q_CHUNK
    q_ch = q.reshape(B, n_ch, Q_CHUNK, N_HEADS, HEAD_DIM).transpose(1, 0, 2, 3, 4)
    idx_ch = idx.reshape(B, n_ch, Q_CHUNK, N_HEADS, J).transpose(1, 0, 2, 3, 4)
    bi = jnp.arange(B)[:, None, None, None]
    hi = jnp.arange(N_HEADS)[None, None, :, None]

    def chunk(args):
        q_c, idx_c, c = args
        t_c = c * Q_CHUNK + jnp.arange(Q_CHUNK)
        ob = (c * Q_CHUNK) // MOBA_BLOCK
        k_sel = kb[bi, hi, idx_c]
        v_sel = vb[bi, hi, idx_c]
        sel_pos = idx_c[..., None] * MOBA_BLOCK + jnp.arange(MOBA_BLOCK)
        sel_ok = (jnp.arange(J)[None, :] < (t_c // MOBA_BLOCK)[:, None])[None, :, None, :, None]
        k_loc = lax.dynamic_index_in_dim(kb, ob, axis=2, keepdims=False)
        v_loc = lax.dynamic_index_in_dim(vb, ob, axis=2, keepdims=False)
        loc_pos = ob * MOBA_BLOCK + jnp.arange(MOBA_BLOCK)
        loc_ok = (loc_pos[None, :] <= t_c[:, None])[None, :, None, :]
        return moba_core(q_c, t_c, k_sel, v_sel, sel_pos, sel_ok, k_loc, v_loc, loc_pos, loc_ok)

    out = lax.map(chunk, (q_ch, idx_ch, jnp.arange(n_ch)))
    return out.transpose(1, 0, 2, 3, 4).reshape(B, S, N_HEADS, HEAD_DIM)


def moba_sample(q, k, v, cache_k, cache_v, page_table):
    DB, T = q.shape[0], q.shape[1]
    n_pages = page_table.shape[1]
    ppb = MOBA_BLOCK // PAGE_SIZE
    n_pb = PAST_LEN // MOBA_BLOCK
    J = min(MOBA_TOPK, n_pb)
    t_q = PAST_LEN + jnp.arange(T)
    if J > 0:
        pk = cache_k[page_table[:, :n_pb * ppb]]
        kmean = jnp.mean(pk.astype(jnp.float32), axis=3).reshape(DB, n_pb, ppb, N_HEADS, HEAD_DIM).mean(axis=2)
        scores = jnp.einsum('bthd,bnhd->bthn', q, kmean, preferred_element_type=jnp.float32)
        _, idx = lax.top_k(scores, J)
    else:
        idx = jnp.zeros((DB, T, N_HEADS, 0), jnp.int32)
    bi = jnp.arange(DB)[:, None, None, None, None]
    hi = jnp.arange(N_HEADS)[None, None, :, None, None]
    phys = page_table[bi, idx[..., None] * ppb + jnp.arange(ppb)]
    k_sel = cache_k[phys, hi].reshape(DB, T, N_HEADS, J, MOBA_BLOCK, HEAD_DIM)
    v_sel = cache_v[phys, hi].reshape(DB, T, N_HEADS, J, MOBA_BLOCK, HEAD_DIM)
    sel_pos = idx[..., None] * MOBA_BLOCK + jnp.arange(MOBA_BLOCK)
    sel_ok = jnp.ones((), jnp.bool_)
    lp_own = np.minimum(n_pb * ppb + np.arange(ppb), n_pages - 1)
    phys_own = page_table[:, lp_own]
    k_own = cache_k[phys_own].transpose(0, 2, 1, 3, 4).reshape(DB, N_HEADS, MOBA_BLOCK, HEAD_DIM)
    v_own = cache_v[phys_own].transpose(0, 2, 1, 3, 4).reshape(DB, N_HEADS, MOBA_BLOCK, HEAD_DIM)
    k_loc = jnp.concatenate([k_own, k.transpose(0, 2, 1, 3)], axis=2)
    v_loc = jnp.concatenate([v_own, v.transpose(0, 2, 1, 3)], axis=2)
    loc_pos = jnp.concatenate([n_pb * MOBA_BLOCK + jnp.arange(MOBA_BLOCK), t_q])
    is_new = jnp.concatenate([jnp.zeros((MOBA_BLOCK,), jnp.bool_), jnp.ones((T,), jnp.bool_)])
    loc_ok = jnp.where(is_new[None, :], loc_pos[None, :] <= t_q[:, None], loc_pos[None, :] < PAST_LEN)
    return moba_core(q, t_q, k_sel, v_sel, sel_pos, sel_ok, k_loc, v_loc, loc_pos, loc_ok[None, :, None, :])


def hybrid_layer(x, u_prev, mem_k, mem_v, moba_fn, w):
    B, T, _ = x.shape
    xn = rmsnorm(x, w['g_mix'])
    z = xn @ w['w_in']
    splits = [int(s) for s in np.cumsum(IN_SIZES)[:-1]]
    glu_a, glu_b, q, k, v, q_mem, g_a, g_b, g_c = jnp.split(z, splits, axis=-1)
    u = glu_a * jax.nn.sigmoid(glu_b)
    u_hist = jnp.concatenate([u_prev, u], axis=1)
    c = depthwise_causal_conv(u_hist, w['w_dw'], w['b_dw'])
    c = layernorm(c, w['ln_conv_g'], w['ln_conv_b'])
    y_a = jax.nn.silu(c) @ w['w_conv_out']
    q = q.reshape(B, T, N_HEADS, HEAD_DIM)
    k = k.reshape(B, T, N_HEADS, HEAD_DIM)
    v = v.reshape(B, T, N_HEADS, HEAD_DIM)
    y_b = moba_fn(q, k, v).reshape(B, T, ATTN_W) @ w['w_attn_out']
    y_c = mem_attend(q_mem.reshape(B, T, MEM_HEADS, MEM_HEAD_DIM), mem_k, mem_v).reshape(B, T, MEM_W) @ w['w_mem_out']
    mix = jax.nn.sigmoid(g_a) * y_a + jax.nn.sigmoid(g_b) * y_b + jax.nn.sigmoid(g_c) * y_c
    h = x + mix @ w['w_out']
    f = rmsnorm(h, w['g_ffn']) @ w['w_up']
    h = h + jnp.square(jax.nn.relu(f)) @ w['w_down']
    return h, u_hist[:, -(CONV_WIDTH - 1):], k, v


def setup_inputs(seed: int = 0) -> dict:
    key = jax.random.key(seed)
    ks = jax.random.split(key, 32)
    f32 = jnp.float32
    n_pages = PAST_LEN // PAGE_SIZE
    n_used = DEC_BATCH * n_pages
    n_pool = n_used + (n_used + 3) // 4

    def nrm(k, shape, scale):
        return jax.random.normal(k, shape, f32) * scale

    page_table = jax.random.permutation(ks[5], n_pool)[:n_used].reshape(DEC_BATCH, n_pages).astype(jnp.int32)
    return {
        'x_prompt': nrm(ks[0], (BATCH, SEQ, D_MODEL), 1.0),
        'x_sample': nrm(ks[1], (DEC_BATCH, DEC_SEQ, D_MODEL), 1.0),
        'mem_prompt': nrm(ks[2], (BATCH, N_MEM, D_MODEL), 1.0),
        'cache_k': nrm(ks[3], (n_pool, N_HEADS, PAGE_SIZE, HEAD_DIM), 1.0),
        'cache_v': nrm(ks[4], (n_pool, N_HEADS, PAGE_SIZE, HEAD_DIM), 1.0),
        'page_table': page_table,
        'cache_mem_k': nrm(ks[6], (DEC_BATCH, N_MEM, MEM_HEADS, MEM_HEAD_DIM), 1.0),
        'cache_mem_v': nrm(ks[7], (DEC_BATCH, N_MEM, MEM_HEADS, MEM_HEAD_DIM), 1.0),
        'state_conv': nrm(ks[8], (DEC_BATCH, CONV_WIDTH - 1, CONV_CH), 0.5),
        'g_mix': 1.0 + nrm(ks[9], (D_MODEL,), 0.05),
        'w_in': nrm(ks[10], (D_MODEL, D_IN), D_MODEL ** -0.5),
        'w_dw': nrm(ks[11], (CONV_WIDTH, CONV_CH), CONV_WIDTH ** -0.5),
        'b_dw': nrm(ks[12], (CONV_CH,), 0.02),
        'ln_conv_g': 1.0 + nrm(ks[13], (CONV_CH,), 0.05),
        'ln_conv_b': nrm(ks[14], (CONV_CH,), 0.02),
        'w_conv_out': nrm(ks[15], (CONV_CH, D_MODEL), CONV_CH ** -0.5),
        'w_attn_out': nrm(ks[16], (ATTN_W, D_MODEL), ATTN_W ** -0.5),
        'g_mem': 1.0 + nrm(ks[17], (D_MODEL,), 0.05),
        'w_mem_k': nrm(ks[18], (D_MODEL, MEM_W), D_MODEL ** -0.5),
        'w_mem_v': nrm(ks[19], (D_MODEL, MEM_W), D_MODEL ** -0.5),
        'w_mem_out': nrm(ks[20], (MEM_W, D_MODEL), MEM_W ** -0.5),
        'w_out': nrm(ks[21], (D_MODEL, D_MODEL), D_MODEL ** -0.5),
        'g_ffn': 1.0 + nrm(ks[22], (D_MODEL,), 0.05),
        'w_up': nrm(ks[23], (D_MODEL, D_FF), D_MODEL ** -0.5),
        'w_down': nrm(ks[24], (D_FF, D_MODEL), D_FF ** -0.5),
        'g_final': 1.0 + nrm(ks[25], (D_MODEL,), 0.05),
    }


def reference(x_prompt, x_sample, mem_prompt, cache_k, cache_v, page_table, cache_mem_k, cache_mem_v, state_conv,
              g_mix, w_in, w_dw, b_dw, ln_conv_g, ln_conv_b, w_conv_out, w_attn_out, g_mem, w_mem_k, w_mem_v,
              w_mem_out, w_out, g_ffn, w_up, w_down, g_final):
    w = dict(g_mix=g_mix, w_in=w_in, w_dw=w_dw, b_dw=b_dw, ln_conv_g=ln_conv_g, ln_conv_b=ln_conv_b,
             w_conv_out=w_conv_out, w_attn_out=w_attn_out, w_mem_out=w_mem_out, w_out=w_out,
             g_ffn=g_ffn, w_up=w_up, w_down=w_down)
    B, S = x_prompt.shape[0], x_prompt.shape[1]
    DB, T = x_sample.shape[0], x_sample.shape[1]

    mn = rmsnorm(mem_prompt, g_mem)
    mem_k_p = (mn @ w_mem_k).reshape(B, N_MEM, MEM_HEADS, MEM_HEAD_DIM)
    mem_v_p = (mn @ w_mem_v).reshape(B, N_MEM, MEM_HEADS, MEM_HEAD_DIM)
    u0 = jnp.zeros((B, CONV_WIDTH - 1, CONV_CH), x_prompt.dtype)
    h_p, conv_p, k_p, v_p = hybrid_layer(x_prompt, u0, mem_k_p, mem_v_p, moba_prompt, w)
    y_prompt = rmsnorm(h_p, g_final)
    new_k_prompt = k_p.reshape(B, S // PAGE_SIZE, PAGE_SIZE, N_HEADS, HEAD_DIM).transpose(0, 1, 3, 2, 4)
    new_v_prompt = v_p.reshape(B, S // PAGE_SIZE, PAGE_SIZE, N_HEADS, HEAD_DIM).transpose(0, 1, 3, 2, 4)

    moba_s = lambda q, k, v: moba_sample(q, k, v, cache_k, cache_v, page_table)
    h_s, conv_s, k_s, v_s = hybrid_layer(x_sample, state_conv, cache_mem_k, cache_mem_v, moba_s, w)
    y_sample = rmsnorm(h_s, g_final)
    new_k_sample = k_s.transpose(0, 2, 1, 3)
    new_v_sample = v_s.transpose(0, 2, 1, 3)

    return (y_prompt, y_sample, new_k_prompt, new_v_prompt, conv_p, mem_k_p, mem_v_p,
            new_k_sample, new_v_sample, conv_s)
```

```python
import functools

import numpy as np
import jax
import jax.numpy as jnp
from jax import lax
from jax.experimental import pallas as pl
from jax.experimental.pallas import tpu as pltpu

F32 = jnp.float32
BF16 = jnp.bfloat16

MOBA_BLOCK = 256
MOBA_TOPK = 3
EPS = 1e-6
NEG = -1e30

V7X_VMEM_BYTES = 64 * 1024 * 1024
VMEM_RESERVE_BYTES = 8 * 1024 * 1024
F32_SUBLANES = 8
LANES = 128

_NT = (((1,), (1,)), ((), ()))


def _params(semantics, block_bytes):
    limit = min(int(block_bytes) + VMEM_RESERVE_BYTES, V7X_VMEM_BYTES - VMEM_RESERVE_BYTES)
    return pltpu.CompilerParams(dimension_semantics=semantics, vmem_limit_bytes=limit)


def _nbytes(shape, dtype):
    return int(np.prod(shape)) * jnp.dtype(dtype).itemsize


def _sigmoid(x):
    return 1.0 / (1.0 + jnp.exp(-x))


def _tile(n, cap):
    t = min(n, cap)
    while n % t:
        t -= 1
    return t


def _rms_cast_kernel(x_ref, g_ref, o_ref):
    x = x_ref[...]
    y = x * lax.rsqrt(jnp.mean(x * x, axis=-1, keepdims=True) + EPS)
    o_ref[...] = (y * g_ref[...]).astype(o_ref.dtype)


def rms_cast(x, g):
    m, d = x.shape
    tm = _tile(m, 512)
    blocks = 2 * (_nbytes((tm, d), F32) + _nbytes((tm, d), BF16)) + 2 * _nbytes((tm, d), F32)
    return pl.pallas_call(
        _rms_cast_kernel,
        out_shape=jax.ShapeDtypeStruct((m, d), BF16),
        grid=(m // tm,),
        in_specs=[pl.BlockSpec((tm, d), lambda i: (i, 0)), pl.BlockSpec((1, d), lambda i: (0, 0))],
        out_specs=pl.BlockSpec((tm, d), lambda i: (i, 0)),
        compiler_params=_params(("parallel",), blocks),
        name="rms_cast",
    )(x, g.reshape(1, d))


def _proj_kernel(x_ref, w_ref, o_ref):
    o_ref[...] = jnp.dot(x_ref[...], w_ref[...], preferred_element_type=F32).astype(o_ref.dtype)


def proj(x, w, col0, n, out_dtype, *, tm_cap=1024, tn_cap=512):
    m, k = x.shape
    tm, tn = _tile(m, tm_cap), _tile(n, tn_cap)
    assert col0 % tn == 0
    blocks = 2 * (_nbytes((tm, k), BF16) + _nbytes((k, tn), BF16) + _nbytes((tm, tn), out_dtype)) + _nbytes((tm, tn), F32)
    return pl.pallas_call(
        _proj_kernel,
        out_shape=jax.ShapeDtypeStruct((m, n), out_dtype),
        grid=(m // tm, n // tn),
        in_specs=[pl.BlockSpec((tm, k), lambda i, j: (i, 0)),
                  pl.BlockSpec((k, tn), lambda i, j: (0, j + col0 // tn))],
        out_specs=pl.BlockSpec((tm, tn), lambda i, j: (i, j)),
        compiler_params=_params(("parallel", "arbitrary"), blocks),
        name="proj",
    )(x, w)


def _glu_kernel(x_ref, wa_ref, wb_ref, o_ref):
    x = x_ref[...]
    a = jnp.dot(x, wa_ref[...], preferred_element_type=F32)
    b = jnp.dot(x, wb_ref[...], preferred_element_type=F32)
    o_ref[...] = a * _sigmoid(b)


def glu_proj(x, w, c):
    m, k = x.shape
    tm, tn = _tile(m, 1024), _tile(c, 256)
    blocks = 2 * (_nbytes((tm, k), BF16) + 2 * _nbytes((k, tn), BF16) + _nbytes((tm, tn), F32)) + 3 * _nbytes((tm, tn), F32)
    return pl.pallas_call(
        _glu_kernel,
        out_shape=jax.ShapeDtypeStruct((m, c), F32),
        grid=(m // tm, c // tn),
        in_specs=[pl.BlockSpec((tm, k), lambda i, j: (i, 0)),
                  pl.BlockSpec((k, tn), lambda i, j: (0, j)),
                  pl.BlockSpec((k, tn), lambda i, j: (0, j + c // tn))],
        out_specs=pl.BlockSpec((tm, tn), lambda i, j: (i, j)),
        compiler_params=_params(("parallel", "arbitrary"), blocks),
        name="glu_proj",
    )(x, w, w)


def _kv_proj_kernel(x_ref, w_ref, pages_ref, flat_ref, *mean_ref, page, heads_per_step, head_dim):
    acc = jnp.dot(x_ref[...], w_ref[...], preferred_element_type=F32)
    s = acc.shape[0]
    flat_ref[...] = acc.astype(flat_ref.dtype)
    for p in range(s // page):
        for hh in range(heads_per_step):
            pages_ref[0, p, hh] = acc[p * page:(p + 1) * page, hh * head_dim:(hh + 1) * head_dim]
    if mean_ref:
        nb = s // MOBA_BLOCK
        mean_ref[0][0] = acc.reshape(nb, MOBA_BLOCK, acc.shape[1]).sum(axis=1) * (1.0 / MOBA_BLOCK)


def kv_proj(x, w, col0, batch, seq, n_heads, head_dim, page, with_mean):
    m, k = x.shape
    assert m == batch * seq and seq % MOBA_BLOCK == 0 and seq % page == 0
    hps = 2 if n_heads % 2 == 0 else 1
    tn = hps * head_dim
    assert col0 % tn == 0
    nb = seq // MOBA_BLOCK
    out_shape = [jax.ShapeDtypeStruct((batch, seq // page, n_heads, page, head_dim), F32),
                 jax.ShapeDtypeStruct((m, n_heads * head_dim), BF16)]
    out_specs = [pl.BlockSpec((1, seq // page, hps, page, head_dim), lambda b, j: (b, 0, j, 0, 0)),
                 pl.BlockSpec((seq, tn), lambda b, j: (b, j))]
    if with_mean:
        out_shape.append(jax.ShapeDtypeStruct((batch, nb, n_heads * head_dim), F32))
        out_specs.append(pl.BlockSpec((1, nb, tn), lambda b, j: (b, 0, j)))
    blocks = (2 * (_nbytes((seq, k), BF16) + _nbytes((k, tn), BF16) + _nbytes((seq, tn), F32) + _nbytes((seq, tn), BF16))
              + 2 * _nbytes((seq, tn), F32))
    return pl.pallas_call(
        functools.partial(_kv_proj_kernel, page=page, heads_per_step=hps, head_dim=head_dim),
        out_shape=out_shape,
        grid=(batch, n_heads // hps),
        in_specs=[pl.BlockSpec((seq, k), lambda b, j: (b, 0)),
                  pl.BlockSpec((k, tn), lambda b, j: (0, j + col0 // tn))],
        out_specs=out_specs,
        compiler_params=_params(("parallel", "arbitrary"), blocks),
        name="kv_proj_mean" if with_mean else "kv_proj",
    )(x, w)


def _ln_swish(c, g, b):
    mu = jnp.mean(c, axis=-1, keepdims=True)
    d = c - mu
    var = jnp.mean(d * d, axis=-1, keepdims=True)
    y = d * lax.rsqrt(var + EPS) * g + b
    return y * _sigmoid(y)


def _conv_prompt_kernel(cur_ref, halo_ref, w_ref, b_ref, g_ref, be_ref, o_ref, hist_ref, win_ref, *, width, halo, rows):
    i = pl.program_id(1)
    ts = cur_ref.shape[1]
    prev = halo_ref[0]
    hist_ref[0:halo, :] = jnp.where(i > 0, prev, jnp.zeros_like(prev))
    hist_ref[halo:halo + ts, :] = cur_ref[0]
    first = halo - (width - 1)

    def row_group(r, carry):
        base = pl.multiple_of(r * rows, rows)
        win_ref[...] = hist_ref[pl.ds(base, rows + halo), :]
        acc = jnp.broadcast_to(b_ref[...], (rows, b_ref.shape[1]))
        for j in range(width):
            acc = acc + win_ref[first + j:first + j + rows, :] * w_ref[j:j + 1, :]
        o_ref[pl.ds(base, rows), :] = _ln_swish(acc, g_ref[...], be_ref[...]).astype(o_ref.dtype)
        return carry

    lax.fori_loop(0, ts // rows, row_group, 0)


def conv_prompt(u, w_dw, b_dw, ln_g, ln_b):
    batch, seq, c = u.shape
    width = w_dw.shape[0]
    halo = -(-(width - 1) // F32_SUBLANES) * F32_SUBLANES
    ts = _tile(seq, 256)
    assert ts % halo == 0
    rows = 16
    row = lambda a: a.reshape(1, c)
    blocks = 2 * (_nbytes((ts, c), F32) + _nbytes((halo, c), F32) + _nbytes((ts, c), BF16)) + _nbytes((ts + halo, c), F32)
    return pl.pallas_call(
        functools.partial(_conv_prompt_kernel, width=width, halo=halo, rows=rows),
        out_shape=jax.ShapeDtypeStruct((batch * seq, c), BF16),
        grid=(batch, seq // ts),
        in_specs=[pl.BlockSpec((1, ts, c), lambda b, i: (b, i, 0)),
                  pl.BlockSpec((1, halo, c), lambda b, i: (b, jnp.maximum(i * (ts // halo) - 1, 0), 0)),
                  pl.BlockSpec((width, c), lambda b, i: (0, 0)),
                  pl.BlockSpec((1, c), lambda b, i: (0, 0)),
                  pl.BlockSpec((1, c), lambda b, i: (0, 0)),
                  pl.BlockSpec((1, c), lambda b, i: (0, 0))],
        out_specs=pl.BlockSpec((ts, c), lambda b, i: (b * (seq // ts) + i, 0)),
        scratch_shapes=[pltpu.VMEM((ts + halo, c), F32), pltpu.VMEM((rows + halo, c), F32)],
        compiler_params=_params(("parallel", "arbitrary"), blocks),
        name="conv_prompt",
    )(u, u, w_dw, row(b_dw), row(ln_g), row(ln_b))


def _conv_sample_kernel(state_ref, u_ref, w_ref, b_ref, g_ref, be_ref, a_ref, newstate_ref, *, width):
    st = state_ref[0]
    u = u_ref[0]
    c = jnp.sum(st * w_ref[0:width - 1, :], axis=0, keepdims=True) + u * w_ref[width - 1:width, :] + b_ref[...]
    a_ref[0] = _ln_swish(c, g_ref[...], be_ref[...]).astype(a_ref.dtype)
    newstate_ref[0, 0:width - 2, :] = st[1:width - 1, :]
    newstate_ref[0, width - 2:width - 1, :] = u


def conv_sample(state, u, w_dw, b_dw, ln_g, ln_b):
    db, wm1, c = state.shape
    width = w_dw.shape[0]
    assert wm1 == width - 1 and u.shape == (db, 1, c)
    row = lambda a: a.reshape(1, c)
    vec = pl.BlockSpec((1, c), lambda b: (0, 0))
    blocks = 4 * _nbytes((1, 32, c), F32) + 4 * _nbytes((32, c), F32)
    return pl.pallas_call(
        functools.partial(_conv_sample_kernel, width=width),
        out_shape=[jax.ShapeDtypeStruct((db, 1, c), BF16), jax.ShapeDtypeStruct((db, wm1, c), F32)],
        grid=(db,),
        in_specs=[pl.BlockSpec((1, wm1, c), lambda b: (b, 0, 0)),
                  pl.BlockSpec((1, 1, c), lambda b: (b, 0, 0)),
                  pl.BlockSpec((width, c), lambda b: (0, 0)), vec, vec, vec],
        out_specs=[pl.BlockSpec((1, 1, c), lambda b: (b, 0, 0)),
                   pl.BlockSpec((1, wm1, c), lambda b: (b, 0, 0))],
        compiler_params=_params(("parallel",), blocks),
        name="conv_sample",
    )(state, u, w_dw, row(b_dw), row(ln_g), row(ln_b))


def _moba_prompt_kernel(slopes_ref, q_ref, k_ref, v_ref, km_ref, o_ref, *, n_blocks, topk, scale):
    h = pl.program_id(1)
    qb = pl.program_id(2)
    blk = MOBA_BLOCK
    slope = slopes_ref[h]
    q = q_ref[...]
    row = lax.broadcasted_iota(jnp.int32, (blk, blk), 0)
    col = lax.broadcasted_iota(jnp.int32, (blk, blk), 1)
    rel = col - row

    def logits(n, i):
        kn = k_ref[n * blk:(n + 1) * blk, :]
        lg = lax.dot_general(q, kn, _NT, preferred_element_type=F32) * scale
        return lg + slope * (rel + (n - i) * blk).astype(F32)

    def attend(i):
        parts = [jnp.where(rel <= 0, logits(i, i), NEG)]
        if i > 0:
            kmb = km_ref[0].astype(BF16)
            s = lax.dot_general(q, kmb, _NT, preferred_element_type=F32)
            cols = [s[:, n:n + 1] for n in range(i)]
            for n in range(i):
                lg = logits(n, i)
                if i > topk:
                    beaten = jnp.zeros((blk, 1), F32)
                    for m in range(i):
                        if m != n:
                            wins = (cols[m] >= cols[n]) if m < n else (cols[m] > cols[n])
                            beaten = beaten + jnp.where(wins, 1.0, 0.0)
                    lg = jnp.where(beaten < topk, lg, NEG)
                parts.append(lg)
        blocks = [i] + list(range(i))
        mx = functools.reduce(jnp.maximum, [p.max(axis=-1, keepdims=True) for p in parts])
        den = jnp.zeros((blk, 1), F32)
        acc = jnp.zeros((blk, q.shape[1]), F32)
        for n, lg in zip(blocks, parts):
            p = jnp.exp(lg - mx)
            den = den + p.sum(axis=-1, keepdims=True)
            acc = acc + jnp.dot(p.astype(BF16), v_ref[n * blk:(n + 1) * blk, :], preferred_element_type=F32)
        o_ref[...] = (acc / den).astype(o_ref.dtype)

    for i in range(n_blocks):
        pl.when(qb == i)(functools.partial(attend, i))


def moba_prompt(q, k, v, kmean, slopes, batch, seq, n_heads, head_dim):
    nb = seq // MOBA_BLOCK
    topk = min(MOBA_TOPK, nb - 1)
    blocks = 2 * (2 * _nbytes((MOBA_BLOCK, head_dim), BF16) + 2 * _nbytes((seq, head_dim), BF16)) + 24 * _nbytes((MOBA_BLOCK, MOBA_BLOCK), F32)
    return pl.pallas_call(
        functools.partial(_moba_prompt_kernel, n_blocks=nb, topk=topk, scale=head_dim ** -0.5),
        out_shape=jax.ShapeDtypeStruct((batch * seq, n_heads * head_dim), BF16),
        grid=(batch, n_heads, nb),
        in_specs=[pl.BlockSpec(memory_space=pltpu.SMEM),
                  pl.BlockSpec((MOBA_BLOCK, head_dim), lambda b, h, i: (b * nb + i, h)),
                  pl.BlockSpec((seq, head_dim), lambda b, h, i: (b, h)),
                  pl.BlockSpec((seq, head_dim), lambda b, h, i: (b, h)),
                  pl.BlockSpec((1, nb, head_dim), lambda b, h, i: (b, 0, h))],
        out_specs=pl.BlockSpec((MOBA_BLOCK, head_dim), lambda b, h, i: (b * nb + i, h)),
        compiler_params=_params(("parallel", "parallel", "arbitrary"), blocks),
        name="moba_prompt",
    )(slopes, q, k, v, kmean)


def _mem_attn_kernel(q_ref, k_ref, v_ref, o_ref, *, scale, min_rows):
    q = q_ref[0]
    tq = q.shape[0]
    if tq < min_rows:
        q = jnp.broadcast_to(q[0:1, :], (min_rows, q.shape[1]))
    k = k_ref[0].astype(BF16)
    v = v_ref[0].astype(BF16)
    lg = lax.dot_general(q, k, _NT, preferred_element_type=F32) * scale
    p = jnp.exp(lg - lg.max(axis=-1, keepdims=True))
    den = p.sum(axis=-1, keepdims=True)
    out = jnp.dot(p.astype(BF16), v, preferred_element_type=F32) / den
    o_ref[0] = out[0:tq, :].astype(o_ref.dtype)


def mem_attn(q, mk, mv, mem_heads):
    bq, t, w = q.shape
    n_mem = mk.shape[1]
    hd = w // mem_heads
    assert t == 1 or t % 16 == 0
    tq = _tile(t, 512)
    blocks = 2 * (2 * _nbytes((tq, hd), BF16) + 2 * _nbytes((n_mem, hd), F32)) + 4 * _nbytes((max(tq, 16), n_mem), F32)
    return pl.pallas_call(
        functools.partial(_mem_attn_kernel, scale=hd ** -0.5, min_rows=16),
        out_shape=jax.ShapeDtypeStruct((bq, t, w), BF16),
        grid=(bq, mem_heads, t // tq),
        in_specs=[pl.BlockSpec((1, tq, hd), lambda b, h, i: (b, i, h)),
                  pl.BlockSpec((1, n_mem, hd), lambda b, h, i: (b, 0, h)),
                  pl.BlockSpec((1, n_mem, hd), lambda b, h, i: (b, 0, h))],
        out_specs=pl.BlockSpec((1, tq, hd), lambda b, h, i: (b, i, h)),
        compiler_params=_params(("parallel", "parallel", "arbitrary"), blocks),
        name="mem_attn",
    )(q, mk, mv)


def _merge_kernel(xn_ref, a_ref, b_ref, c_ref, wga_ref, wgb_ref, wgc_ref, wa_ref, wb_ref, wc_ref, o_ref):
    xn = xn_ref[...]

    def branch(y_ref, w_ref, wg_ref):
        gate = _sigmoid(jnp.dot(xn, wg_ref[...], preferred_element_type=F32))
        return gate * jnp.dot(y_ref[...], w_ref[...], preferred_element_type=F32)

    mix = branch(a_ref, wa_ref, wga_ref) + branch(b_ref, wb_ref, wgb_ref) + branch(c_ref, wc_ref, wgc_ref)
    o_ref[...] = mix.astype(o_ref.dtype)


def merge(xn, ya, yb, yc, w_in, gate_col0, w_a, w_b, w_c):
    m, d = xn.shape
    ka, kb, kc = ya.shape[1], yb.shape[1], yc.shape[1]
    tm, tn = _tile(m, 1024), _tile(d, 256)
    assert gate_col0 % tn == 0
    g0 = gate_col0 // tn
    blocks = (2 * (_nbytes((tm, d + ka + kb + kc), BF16) + _nbytes((3 * d + ka + kb + kc, tn), BF16) + _nbytes((tm, tn), BF16))
              + 6 * _nbytes((tm, tn), F32))
    rows = lambda kk: pl.BlockSpec((tm, kk), lambda i, j: (i, 0))
    wcol = lambda kk, off: pl.BlockSpec((kk, tn), lambda i, j: (0, j + off))
    return pl.pallas_call(
        _merge_kernel,
        out_shape=jax.ShapeDtypeStruct((m, d), BF16),
        grid=(m // tm, d // tn),
        in_specs=[rows(d), rows(ka), rows(kb), rows(kc),
                  wcol(d, g0), wcol(d, g0 + d // tn), wcol(d, g0 + 2 * (d // tn)),
                  wcol(ka, 0), wcol(kb, 0), wcol(kc, 0)],
        out_specs=pl.BlockSpec((tm, tn), lambda i, j: (i, j)),
        compiler_params=_params(("parallel", "arbitrary"), blocks),
        name="merge",
    )(xn, ya, yb, yc, w_in, w_in, w_in, w_a, w_b, w_c)


def _out_proj_kernel(x_ref, mix_ref, w_ref, o_ref):
    o_ref[...] = x_ref[...] + jnp.dot(mix_ref[...], w_ref[...], preferred_element_type=F32)


def out_proj(x, mix, w):
    m, d = x.shape
    tm, tn = _tile(m, 1024), _tile(d, 512)
    blocks = 2 * (_nbytes((tm, d), BF16) + _nbytes((d, tn), BF16) + 2 * _nbytes((tm, tn), F32)) + _nbytes((tm, tn), F32)
    return pl.pallas_call(
        _out_proj_kernel,
        out_shape=jax.ShapeDtypeStruct((m, d), F32),
        grid=(m // tm, d // tn),
        in_specs=[pl.BlockSpec((tm, tn), lambda i, j: (i, j)),
                  pl.BlockSpec((tm, d), lambda i, j: (i, 0)),
                  pl.BlockSpec((d, tn), lambda i, j: (0, j))],
        out_specs=pl.BlockSpec((tm, tn), lambda i, j: (i, j)),
        compiler_params=_params(("parallel", "arbitrary"), blocks),
        name="out_proj",
    )(x, mix, w)


def _ffn_kernel(h_ref, g_ref, gf_ref, wu_ref, wd_ref, o_ref, hn_ref, acc_ref):
    kk = pl.program_id(1)

    @pl.when(kk == 0)
    def _():
        h = h_ref[...]
        y = h * lax.rsqrt(jnp.mean(h * h, axis=-1, keepdims=True) + EPS)
        hn_ref[...] = (y * g_ref[...]).astype(hn_ref.dtype)
        acc_ref[...] = jnp.zeros_like(acc_ref)

    f = jnp.maximum(jnp.dot(hn_ref[...], wu_ref[...], preferred_element_type=F32), 0.0)
    acc_ref[...] += jnp.dot((f * f).astype(BF16), wd_ref[...], preferred_element_type=F32)

    @pl.when(kk == pl.num_programs(1) - 1)
    def _():
        h = h_ref[...] + acc_ref[...]
        y = h * lax.rsqrt(jnp.mean(h * h, axis=-1, keepdims=True) + EPS)
        o_ref[...] = y * gf_ref[...]


def ffn(h, g_ffn, g_final, w_up, w_down):
    m, d = h.shape
    ff = w_up.shape[1]
    tm, tf = _tile(m, 512), _tile(ff, 512)
    blocks = (2 * (2 * _nbytes((tm, d), F32) + 2 * _nbytes((d, tf), BF16)) + _nbytes((tm, d), BF16) + _nbytes((tm, d), F32)
              + 2 * _nbytes((tm, tf), F32) + _nbytes((tm, d), F32))
    return pl.pallas_call(
        _ffn_kernel,
        out_shape=jax.ShapeDtypeStruct((m, d), F32),
        grid=(m // tm, ff // tf),
        in_specs=[pl.BlockSpec((tm, d), lambda i, k: (i, 0)),
                  pl.BlockSpec((1, d), lambda i, k: (0, 0)),
                  pl.BlockSpec((1, d), lambda i, k: (0, 0)),
                  pl.BlockSpec((d, tf), lambda i, k: (0, k)),
                  pl.BlockSpec((tf, d), lambda i, k: (k, 0))],
        out_specs=pl.BlockSpec((tm, d), lambda i, k: (i, 0)),
        scratch_shapes=[pltpu.VMEM((tm, d), BF16), pltpu.VMEM((tm, d), F32)],
        compiler_params=_params(("parallel", "arbitrary"), blocks),
        name="ffn",
    )(h, g_ffn.reshape(1, d), g_final.reshape(1, d), w_up, w_down)


def _page_mean_kernel(pt_ref, *refs, pages_per_step, pages_per_block):
    page_refs, o_ref = refs[:pages_per_step], refs[pages_per_step]
    rows = page_refs[0].shape[2]
    for blk in range(pages_per_step // pages_per_block):
        tot = page_refs[blk * pages_per_block][0].sum(axis=1)
        for pp in range(1, pages_per_block):
            tot = tot + page_refs[blk * pages_per_block + pp][0].sum(axis=1)
        o_ref[0, blk] = tot * (1.0 / (rows * pages_per_block))


def page_block_means(cache_k, page_table, n_blocks, pages_per_block):
    db, n_pages = page_table.shape
    _, n_heads, ps, hd = cache_k.shape
    per_step = 8
    assert per_step % pages_per_block == 0 and (n_blocks * pages_per_block) % per_step == 0
    blocks_per_step = per_step // pages_per_block
    steps = n_blocks * pages_per_block // per_step
    page_spec = lambda r: pl.BlockSpec((1, n_heads, ps, hd),
                                       lambda b, g, pt: (pt[b * n_pages + g * per_step + r], 0, 0, 0))
    blocks = 2 * (per_step * _nbytes((n_heads, ps, hd), F32) + _nbytes((blocks_per_step, n_heads, hd), F32))
    return pl.pallas_call(
        functools.partial(_page_mean_kernel, pages_per_step=per_step, pages_per_block=pages_per_block),
        out_shape=jax.ShapeDtypeStruct((db, n_blocks, n_heads, hd), F32),
        grid_spec=pltpu.PrefetchScalarGridSpec(
            num_scalar_prefetch=1,
            grid=(db, steps),
            in_specs=[page_spec(r) for r in range(per_step)],
            out_specs=pl.BlockSpec((1, blocks_per_step, n_heads, hd), lambda b, g, pt: (b, g, 0, 0))),
        compiler_params=_params(("parallel", "arbitrary"), blocks),
        name="page_block_means",
    )(page_table.reshape(-1), *([cache_k] * per_step))


def _block_topk_kernel(q_ref, km_ref, o_ref, *, n_heads, head_dim, topk, min_rows):
    n_blocks = km_ref.shape[1]
    q = q_ref[0]
    lane = lax.broadcasted_iota(jnp.int32, (1, n_blocks), 1)
    for h in range(n_heads):
        qh = jnp.broadcast_to(q[:, h * head_dim:(h + 1) * head_dim], (min_rows, head_dim))
        kmh = km_ref[0, :, h * head_dim:(h + 1) * head_dim].astype(BF16)
        s = lax.dot_general(qh, kmh, _NT, preferred_element_type=F32)[0:1, :]
        for j in range(topk):
            best = jnp.min(jnp.where(s == s.max(axis=-1, keepdims=True), lane, n_blocks), axis=-1, keepdims=True)
            o_ref[0, h:h + 1, j:j + 1] = best
            s = jnp.where(lane == best, -jnp.inf, s)


def block_topk(q, kmean, n_heads, head_dim, topk):
    db, n_blocks, w = kmean.shape
    blocks = 2 * (_nbytes((n_blocks, w), F32) + _nbytes((16, w), BF16) + _nbytes((8, 128), jnp.int32)) + 8 * _nbytes((16, 128), F32)
    return pl.pallas_call(
        functools.partial(_block_topk_kernel, n_heads=n_heads, head_dim=head_dim, topk=topk, min_rows=16),
        out_shape=jax.ShapeDtypeStruct((db, n_heads, topk), jnp.int32),
        grid=(db,),
        in_specs=[pl.BlockSpec((1, 1, w), lambda b: (b, 0, 0)),
                  pl.BlockSpec((1, n_blocks, w), lambda b: (b, 0, 0))],
        out_specs=pl.BlockSpec((1, n_heads, topk), lambda b: (b, 0, 0)),
        compiler_params=_params(("parallel",), blocks),
        name="block_topk",
    )(q, kmean)


def _moba_sample_kernel(pt_ref, idx_ref, slopes_ref, q_ref, kn_ref, vn_ref, *refs,
                        n_sel, pages_per_block, past_len, scale, min_rows):
    n_sel_pages = n_sel * pages_per_block
    ksel, vsel = refs[:n_sel_pages], refs[n_sel_pages:2 * n_sel_pages]
    kown = refs[2 * n_sel_pages:2 * n_sel_pages + pages_per_block]
    vown = refs[2 * n_sel_pages + pages_per_block:2 * n_sel_pages + 2 * pages_per_block]
    o_ref = refs[2 * n_sel_pages + 2 * pages_per_block]
    b, h = pl.program_id(0), pl.program_id(1)
    n_heads = pl.num_programs(1)
    slope = slopes_ref[h]
    ps, hd = ksel[0].shape[2], ksel[0].shape[3]
    q = jnp.broadcast_to(q_ref[0], (min_rows, hd))
    t_q = past_len
    lane = lax.broadcasted_iota(jnp.int32, (1, ps), 1)

    def page_logits(k_ref, first_pos):
        lg = lax.dot_general(q, k_ref[0, 0].astype(BF16), _NT, preferred_element_type=F32)[0:1, :] * scale
        pos = first_pos + lane
        return lg - slope * (t_q - pos).astype(F32), pos

    parts, values = [], []
    for j in range(n_sel):
        blk = idx_ref[(b * n_heads + h) * n_sel + j]
        for pp in range(pages_per_block):
            lg, _ = page_logits(ksel[j * pages_per_block + pp], blk * MOBA_BLOCK + pp * ps)
            parts.append(lg)
            values.append(vsel[j * pages_per_block + pp])
    own_first = (past_len // MOBA_BLOCK) * MOBA_BLOCK
    for pp in range(pages_per_block):
        lg, pos = page_logits(kown[pp], own_first + pp * ps)
        parts.append(jnp.where(pos < past_len, lg, NEG))
        values.append(vown[pp])
    kn = jnp.broadcast_to(kn_ref[0].astype(BF16), (min_rows, hd))
    new_lane = lax.broadcasted_iota(jnp.int32, (1, min_rows), 1)
    lg_new = lax.dot_general(q, kn, _NT, preferred_element_type=F32)[0:1, :] * scale
    lg_new = jnp.where(new_lane == 0, lg_new, NEG)

    mx = functools.reduce(jnp.maximum, [p.max(axis=-1, keepdims=True) for p in parts + [lg_new]])
    p_new = jnp.exp(lg_new - mx)
    den = p_new.sum(axis=-1, keepdims=True)
    vn = jnp.broadcast_to(vn_ref[0].astype(BF16), (min_rows, hd))
    wide = lambda p: jnp.broadcast_to(p, (min_rows, p.shape[1])).astype(BF16)
    acc = jnp.dot(wide(p_new), vn, preferred_element_type=F32)
    for lg, v_ref in zip(parts, values):
        p = jnp.exp(lg - mx)
        den = den + p.sum(axis=-1, keepdims=True)
        acc = acc + jnp.dot(wide(p), v_ref[0, 0].astype(BF16), preferred_element_type=F32)
    o_ref[0] = (acc[0:1, :] / den).astype(o_ref.dtype)


def moba_sample(q, k_new, v_new, cache_k, cache_v, page_table, idx, slopes, n_heads, head_dim):
    db, n_pages = page_table.shape
    ps = cache_k.shape[2]
    ppb = MOBA_BLOCK // ps
    past_len = n_pages * ps
    n_pb = past_len // MOBA_BLOCK
    n_sel = idx.shape[2]
    sel_spec = lambda j, pp: pl.BlockSpec(
        (1, 1, ps, head_dim),
        lambda b, h, pt, ix: (pt[b * n_pages + ix[(b * n_heads + h) * n_sel + j] * ppb + pp], h, 0, 0))
    own_spec = lambda pp: pl.BlockSpec(
        (1, 1, ps, head_dim),
        lambda b, h, pt, ix: (pt[b * n_pages + min(n_pb * ppb + pp, n_pages - 1)], h, 0, 0))
    row_spec = pl.BlockSpec((1, 1, head_dim), lambda b, h, pt, ix: (b, 0, h))
    sel_specs = [sel_spec(j, pp) for j in range(n_sel) for pp in range(ppb)]
    own_specs = [own_spec(pp) for pp in range(ppb)]
    n_page_inputs = 2 * (len(sel_specs) + len(own_specs))
    blocks = 2 * n_page_inputs * _nbytes((ps, head_dim), F32) + 64 * _nbytes((16, 128), F32)
    return pl.pallas_call(
        functools.partial(_moba_sample_kernel, n_sel=n_sel, pages_per_block=ppb, past_len=past_len,
                          scale=head_dim ** -0.5, min_rows=16),
        out_shape=jax.ShapeDtypeStruct((db, 1, n_heads * head_dim), BF16),
        grid_spec=pltpu.PrefetchScalarGridSpec(
            num_scalar_prefetch=2,
            grid=(db, n_heads),
            in_specs=[pl.BlockSpec(memory_space=pltpu.SMEM), row_spec, row_spec, row_spec]
                     + sel_specs + sel_specs + own_specs + own_specs,
            out_specs=row_spec),
        compiler_params=_params(("parallel", "arbitrary"), blocks),
        name="moba_sample",
    )(page_table.reshape(-1), idx.reshape(-1), slopes, q, k_new, v_new,
      *([cache_k] * len(sel_specs)), *([cache_v] * len(sel_specs)),
      *([cache_k] * len(own_specs)), *([cache_v] * len(own_specs)))


def _alibi_slopes(n_heads):
    return jnp.asarray(2.0 ** (-8.0 * np.arange(1, n_heads + 1) / n_heads), dtype=F32)


def _shared_tail(x, xn, ya, yb, yc, w):
    mix = merge(xn, ya, yb, yc, w["w_in"], w["gate_col0"], w["w_conv_out"], w["w_attn_out"], w["w_mem_out"])
    h = out_proj(x, mix, w["w_out"])
    return ffn(h, w["g_ffn"], w["g_final"], w["w_up"], w["w_down"])


def kernel(x_prompt, x_sample, mem_prompt, cache_k, cache_v, page_table, cache_mem_k, cache_mem_v, state_conv,
           g_mix, w_in, w_dw, b_dw, ln_conv_g, ln_conv_b, w_conv_out, w_attn_out, g_mem, w_mem_k, w_mem_v,
           w_mem_out, w_out, g_ffn, w_up, w_down, g_final):
    batch, seq, d = x_prompt.shape
    db, t_new, _ = x_sample.shape
    assert t_new == 1, "the sample group decodes one new token per sequence"
    conv_ch = w_dw.shape[1]
    _, n_heads, page, head_dim = cache_k.shape
    attn_w = n_heads * head_dim
    n_mem, mem_heads = cache_mem_k.shape[1], cache_mem_k.shape[2]
    mem_w = mem_heads * cache_mem_k.shape[3]
    q0 = 2 * conv_ch
    k0, v0, qm0 = q0 + attn_w, q0 + 2 * attn_w, q0 + 3 * attn_w
    gate0 = qm0 + mem_w
    assert w_in.shape[1] == gate0 + 3 * d

    w = dict(w_in=w_in.astype(BF16), gate_col0=gate0, w_conv_out=w_conv_out.astype(BF16),
             w_attn_out=w_attn_out.astype(BF16), w_mem_out=w_mem_out.astype(BF16), w_out=w_out.astype(BF16),
             w_up=w_up.astype(BF16), w_down=w_down.astype(BF16), g_ffn=g_ffn, g_final=g_final)
    wi = w["w_in"]
    slopes = _alibi_slopes(n_heads)

    xp = x_prompt.reshape(batch * seq, d)
    mn = rms_cast(mem_prompt.reshape(batch * n_mem, d), g_mem)
    w_mem_kv = jnp.concatenate([w_mem_k, w_mem_v], axis=1).astype(BF16)
    mem_k_p = proj(mn, w_mem_kv, 0, mem_w, F32).reshape(batch, n_mem, mem_w)
    mem_v_p = proj(mn, w_mem_kv, mem_w, mem_w, F32).reshape(batch, n_mem, mem_w)

    xn = rms_cast(xp, g_mix)
    u = glu_proj(xn, wi, conv_ch)
    ya = conv_prompt(u.reshape(batch, seq, conv_ch), w_dw, b_dw, ln_conv_g, ln_conv_b)
    q = proj(xn, wi, q0, attn_w, BF16)
    k_pages, k_flat, k_mean = kv_proj(xn, wi, k0, batch, seq, n_heads, head_dim, page, True)
    v_pages, v_flat = kv_proj(xn, wi, v0, batch, seq, n_heads, head_dim, page, False)
    yb = moba_prompt(q, k_flat, v_flat, k_mean, slopes, batch, seq, n_heads, head_dim)
    q_mem = proj(xn, wi, qm0, mem_w, BF16)
    yc = mem_attn(q_mem.reshape(batch, seq, mem_w), mem_k_p, mem_v_p, mem_heads).reshape(batch * seq, mem_w)
    y_prompt = _shared_tail(xp, xn, ya, yb, yc, w).reshape(batch, seq, d)
    new_conv_prompt = u.reshape(batch, seq, conv_ch)[:, seq - (w_dw.shape[0] - 1):, :]

    xs = x_sample.reshape(db, d)
    xn_s = rms_cast(xs, g_mix)
    u_s = glu_proj(xn_s, wi, conv_ch)
    ya_s, new_conv_sample = conv_sample(state_conv, u_s.reshape(db, 1, conv_ch), w_dw, b_dw, ln_conv_g, ln_conv_b)
    q_s = proj(xn_s, wi, q0, attn_w, BF16).reshape(db, 1, attn_w)
    k_s = proj(xn_s, wi, k0, attn_w, F32)
    v_s = proj(xn_s, wi, v0, attn_w, F32)
    ppb = MOBA_BLOCK // page
    n_pb = (page_table.shape[1] * page) // MOBA_BLOCK
    km_s = page_block_means(cache_k, page_table, n_pb, ppb).reshape(db, n_pb, attn_w)
    idx = block_topk(q_s, km_s, n_heads, head_dim, min(MOBA_TOPK, n_pb))
    yb_s = moba_sample(q_s, k_s.reshape(db, 1, attn_w), v_s.reshape(db, 1, attn_w), cache_k, cache_v, page_table, idx,
                       slopes, n_heads, head_dim)
    q_mem_s = proj(xn_s, wi, qm0, mem_w, BF16).reshape(db, 1, mem_w)
    yc_s = mem_attn(q_mem_s, cache_mem_k.reshape(db, n_mem, mem_w), cache_mem_v.reshape(db, n_mem, mem_w), mem_heads)
    y_sample = _shared_tail(xs, xn_s, ya_s.reshape(db, conv_ch), yb_s.reshape(db, attn_w), yc_s.reshape(db, mem_w),
                            w).reshape(db, 1, d)

    return (y_prompt, y_sample, k_pages, v_pages, new_conv_prompt,
            mem_k_p.reshape(batch, n_mem, mem_heads, mem_w // mem_heads),
            mem_v_p.reshape(batch, n_mem, mem_heads, mem_w // mem_heads),
            k_s.reshape(db, n_heads, 1, head_dim), v_s.reshape(db, n_heads, 1, head_dim), new_conv_sample)
```
